```python
import math
import jax, jax.numpy as jnp
from jax import lax
import numpy as np

D_MODEL = 1024
BATCH = 4
SEQ = 8192
DEPTH = 2

D_MIX = D_MODEL
MLA_HEADS = 4
MLA_NOPE = 128
MLA_ROPE = 64
MLA_QK = MLA_NOPE + MLA_ROPE
MLA_V = 128
MLA_Q_RANK = 384
MLA_KV_RANK = 256
MLA_OUT = MLA_HEADS * MLA_V
ROPE_THETA = 10000.0
Q_BLOCK = 128
HG_HEADS = 4
HG_DK = 64
HG_DV = 64
HG_KEY = HG_HEADS * HG_DK
HG_OUT = HG_HEADS * HG_DV
HG_CHUNK = 64
S5_OUT = D_MIX - MLA_OUT - HG_OUT
S5_GROUP = 16
S5_GROUPS = S5_OUT // S5_GROUP
S5_STATE = 64
DT_MIN = 0.001
DT_MAX = 0.1
D_FF = 4 * D_MODEL
IN_WIDTHS = (MLA_Q_RANK, MLA_KV_RANK, MLA_ROPE, HG_KEY, HG_KEY, HG_OUT, HG_OUT, S5_OUT)
D_IN = sum(IN_WIDTHS)
ALPHA = (2 * DEPTH) ** 0.25
BETA = (8 * DEPTH) ** -0.25
LN_EPS = 1e-5
RMS_EPS = 1e-6

kernel_name = "hybrid_mla_hgrn2_s5_deepnorm"


def _layernorm(x, g, b):
    xf = x.astype(jnp.float32)
    mu = jnp.mean(xf, -1, keepdims=True)
    var = jnp.mean(jnp.square(xf - mu), -1, keepdims=True)
    return ((xf - mu) * lax.rsqrt(var + LN_EPS) * g + b).astype(x.dtype)


def _rmsnorm(x, g):
    xf = x.astype(jnp.float32)
    return (xf * lax.rsqrt(jnp.mean(xf * xf, -1, keepdims=True) + RMS_EPS) * g).astype(x.dtype)


def _rope(x, pos):
    r = x.shape[-1]
    freqs = ROPE_THETA ** (-jnp.arange(0, r, 2, dtype=jnp.float32) / r)
    ang = pos[:, None] * freqs[None, :]
    cos = jnp.cos(ang)[:, None, :]
    sin = jnp.sin(ang)[:, None, :]
    xf = x.astype(jnp.float32)
    x1, x2 = xf[..., : r // 2], xf[..., r // 2:]
    return jnp.concatenate([x1 * cos - x2 * sin, x2 * cos + x1 * sin], -1).astype(x.dtype)


def _mla(c_q, c_kv, k_rope, q_norm_g, w_uq, kv_norm_g, w_ukv):
    B, S, _ = c_q.shape
    pos = jnp.arange(S, dtype=jnp.float32)
    q = (_rmsnorm(c_q, q_norm_g) @ w_uq).reshape(B, S, MLA_HEADS, MLA_QK)
    q = jnp.concatenate([q[..., :MLA_NOPE], _rope(q[..., MLA_NOPE:], pos)], -1)
    kv = (_rmsnorm(c_kv, kv_norm_g) @ w_ukv).reshape(B, S, MLA_HEADS, MLA_NOPE + MLA_V)
    k_pe = jnp.broadcast_to(_rope(k_rope[:, :, None, :], pos), (B, S, MLA_HEADS, MLA_ROPE))
    k = jnp.concatenate([kv[..., :MLA_NOPE], k_pe], -1)
    v = kv[..., MLA_NOPE:]
    scale = MLA_QK ** -0.5
    n_blk = S // Q_BLOCK
    q_blocks = q.reshape(B, n_blk, Q_BLOCK, MLA_HEADS, MLA_QK).transpose(1, 0, 2, 3, 4)
    k_pos = jnp.arange(S)

    def attend(args):
        q_blk, start = args
        s = jnp.einsum('bqhd,bkhd->bhqk', q_blk, k).astype(jnp.float32) * scale
        q_pos = start + jnp.arange(Q_BLOCK)
        s = jnp.where(k_pos[None, :] <= q_pos[:, None], s, -jnp.inf)
        p = jax.nn.softmax(s, axis=-1).astype(v.dtype)
        return jnp.einsum('bhqk,bkhd->bqhd', p, v)

    o = lax.map(attend, (q_blocks, jnp.arange(n_blk, dtype=jnp.int32) * Q_BLOCK))
    return o.transpose(1, 0, 2, 3, 4).reshape(B, S, MLA_OUT)


def _hgrn2(q, f_pre, i, g, lb, norm_g):
    B, S, _ = q.shape
    f32 = jnp.float32
    n_chunk = S // HG_CHUNK
    zf = f_pre.astype(f32)
    log_f = jnp.logaddexp(jnp.log(lb), jnp.log1p(-lb) + jax.nn.log_sigmoid(zf))
    k = (1.0 - lb) * jax.nn.sigmoid(-zf)

    def chunks(t, d):
        return t.astype(f32).reshape(B, n_chunk, HG_CHUNK, HG_HEADS, d).transpose(1, 0, 3, 2, 4)

    qc, kc, vc, lfc = chunks(q, HG_DK), chunks(k, HG_DK), chunks(i, HG_DV), chunks(log_f, HG_DK)
    causal = jnp.tril(jnp.ones((HG_CHUNK, HG_CHUNK), dtype=bool))[:, :, None]

    def step(state, xs):
        qb, kb, vb, lfb = xs
        b = jnp.cumsum(lfb, axis=2)
        diff = b[:, :, :, None, :] - b[:, :, None, :, :]
        decay = jnp.exp(jnp.where(causal, diff, -jnp.inf))
        att = jnp.einsum('bhtd,bhsd,bhtsd->bhts', qb, kb, decay)
        o = (jnp.einsum('bhts,bhsv->bhtv', att, vb)
             + jnp.einsum('bhtd,bhdv->bhtv', qb * jnp.exp(b), state))
        b_last = b[:, :, -1:, :]
        state = (jnp.exp(b_last[:, :, 0, :])[..., None] * state
                 + jnp.einsum('bhsd,bhsv->bhdv', kb * jnp.exp(b_last - b), vb))
        return state, o

    s0 = jnp.zeros((B, HG_HEADS, HG_DK, HG_DV), f32)
    _, o = lax.scan(step, s0, (qc, kc, vc, lfc))
    o = o.transpose(1, 0, 3, 2, 4).reshape(B, S, HG_HEADS, HG_DV)
    o = _rmsnorm(o, norm_g) * jax.nn.silu(g.astype(f32).reshape(B, S, HG_HEADS, HG_DV))
    return o.reshape(B, S, HG_OUT).astype(q.dtype)


def _s5(u, a_re, a_im, b_re, b_im, c_re, c_im, d_skip, log_dt, w_glu, b_glu):
    B, S, _ = u.shape
    f32 = jnp.float32
    uf = u.astype(f32).reshape(B, S, S5_GROUPS, S5_GROUP)
    dt = jnp.exp(log_dt.astype(f32))[:, None]
    ar, ai = a_re.astype(f32), a_im.astype(f32)
    mag = jnp.exp(dt * ar)
    abar_re, abar_im = mag * jnp.cos(dt * ai), mag * jnp.sin(dt * ai)
    den = ar * ar + ai * ai
    num_re, num_im = abar_re - 1.0, abar_im
    coef_re = ((num_re * ar + num_im * ai) / den)[..., None]
    coef_im = ((num_im * ar - num_re * ai) / den)[..., None]
    br, bi = b_re.astype(f32), b_im.astype(f32)
    bbar_re = coef_re * br - coef_im * bi
    bbar_im = coef_re * bi + coef_im * br
    x_re = jnp.einsum('gnp,bsgp->bsgn', bbar_re, uf)
    x_im = jnp.einsum('gnp,bsgp->bsgn', bbar_im, uf)
    a_re_t = jnp.broadcast_to(abar_re, x_re.shape)
    a_im_t = jnp.broadcast_to(abar_im, x_re.shape)

    def combine(e1, e2):
        a1r, a1i, b1r, b1i = e1
        a2r, a2i, b2r, b2i = e2
        return (a2r * a1r - a2i * a1i,
                a2r * a1i + a2i * a1r,
                a2r * b1r - a2i * b1i + b2r,
                a2r * b1i + a2i * b1r + b2i)

    _, _, h_re, h_im = lax.associative_scan(combine, (a_re_t, a_im_t, x_re, x_im), axis=1)
    y = (jnp.einsum('gpn,bsgn->bsgp', c_re.astype(f32), h_re)
         - jnp.einsum('gpn,bsgn->bsgp', c_im.astype(f32), h_im)
         + d_skip.astype(f32).reshape(S5_GROUPS, S5_GROUP) * uf)
    y = jax.nn.gelu(y.reshape(B, S, S5_OUT))
    y = y * jax.nn.sigmoid(y @ w_glu.astype(f32) + b_glu.astype(f32))
    return y.astype(u.dtype)


def setup_inputs(seed: int = 0) -> dict:
    key = jax.random.key(seed)
    ks = list(jax.random.split(key, 32))
    L = DEPTH
    G, N, P = S5_GROUPS, S5_STATE, S5_GROUP

    def nrm(j, shape, scale):
        return jax.random.normal(ks[j], shape, jnp.float32) * scale

    n_idx = jnp.arange(N, dtype=jnp.float32)
    return {
        "x": nrm(0, (BATCH, SEQ, D_MODEL), 1.0),
        "ln_in_g": 1.0 + nrm(1, (D_MODEL,), 0.02),
        "ln_in_b": nrm(2, (D_MODEL,), 0.02),
        "w_in": nrm(3, (L, D_MODEL, D_IN), D_MODEL ** -0.5),
        "mla_q_norm_g": 1.0 + nrm(4, (L, MLA_Q_RANK), 0.02),
        "mla_w_uq": nrm(5, (L, MLA_Q_RANK, MLA_HEADS * MLA_QK), MLA_Q_RANK ** -0.5),
        "mla_kv_norm_g": 1.0 + nrm(6, (L, MLA_KV_RANK), 0.02),
        "mla_w_ukv": nrm(7, (L, MLA_KV_RANK, MLA_HEADS * (MLA_NOPE + MLA_V)), MLA_KV_RANK ** -0.5),
        "hg_lower_bound": 1.0 + nrm(8, (L, HG_KEY), 0.1),
        "hg_norm_g": 1.0 + nrm(9, (L, HG_DV), 0.02),
        "s5_a_re": -0.5 + nrm(10, (L, G, N), 0.01),
        "s5_a_im": math.pi * n_idx + nrm(11, (L, G, N), 0.01),
        "s5_b_re": nrm(12, (L, G, N, P), (2.0 * P) ** -0.5),
        "s5_b_im": nrm(13, (L, G, N, P), (2.0 * P) ** -0.5),
        "s5_c_re": nrm(14, (L, G, P, N), (2.0 * N) ** -0.5 * 4.0),
        "s5_c_im": nrm(15, (L, G, P, N), (2.0 * N) ** -0.5 * 4.0),
        "s5_d": nrm(16, (L, S5_OUT), 1.0),
        "s5_log_dt": jax.random.uniform(ks[17], (L, G), jnp.float32, math.log(DT_MIN), math.log(DT_MAX)),
        "s5_w_glu": nrm(18, (L, S5_OUT, S5_OUT), S5_OUT ** -0.5),
        "s5_b_glu": nrm(19, (L, S5_OUT), 0.02),
        "w_out": nrm(20, (L, D_MIX, D_MODEL), D_MIX ** -0.5 * BETA),
        "ln1_g": 1.0 + nrm(21, (L, D_MODEL), 0.02),
        "ln1_b": nrm(22, (L, D_MODEL), 0.02),
        "w_ff1": nrm(23, (L, D_MODEL, D_FF), D_MODEL ** -0.5),
        "w_ff2": nrm(24, (L, D_FF, D_MODEL), D_FF ** -0.5 * BETA),
        "ln2_g": 1.0 + nrm(25, (L, D_MODEL), 0.02),
        "ln2_b": nrm(26, (L, D_MODEL), 0.02),
    }


def reference(x, ln_in_g, ln_in_b, w_in, mla_q_norm_g, mla_w_uq, mla_kv_norm_g, mla_w_ukv,
              hg_lower_bound, hg_norm_g, s5_a_re, s5_a_im, s5_b_re, s5_b_im, s5_c_re, s5_c_im,
              s5_d, s5_log_dt, s5_w_glu, s5_b_glu, w_out, ln1_g, ln1_b, w_ff1, w_ff2,
              ln2_g, ln2_b):
    h = _layernorm(x, ln_in_g, ln_in_b)
    lb_all = jnp.cumsum(jax.nn.softmax(hg_lower_bound.astype(jnp.float32), axis=0), axis=0)
    lb_all = lb_all - lb_all[0]
    offsets = [sum(IN_WIDTHS[:j]) for j in range(1, len(IN_WIDTHS))]
    for l in range(DEPTH):
        proj = h @ w_in[l]
        c_q, c_kv, k_rope, hq, hf, hi, hg, su = jnp.split(proj, offsets, axis=-1)
        o_mla = _mla(c_q, c_kv, k_rope, mla_q_norm_g[l], mla_w_uq[l], mla_kv_norm_g[l], mla_w_ukv[l])
        o_hg = _hgrn2(hq, hf, hi, hg, lb_all[l], hg_norm_g[l])
        o_s5 = _s5(su, s5_a_re[l], s5_a_im[l], s5_b_re[l], s5_b_im[l], s5_c_re[l], s5_c_im[l],
                   s5_d[l], s5_log_dt[l], s5_w_glu[l], s5_b_glu[l])
        mix = jnp.concatenate([o_mla, o_hg, o_s5], axis=-1) @ w_out[l]
        h = _layernorm(ALPHA * h + mix, ln1_g[l], ln1_b[l])
        ff = jnp.square(jax.nn.relu(h @ w_ff1[l])) @ w_ff2[l]
        h = _layernorm(ALPHA * h + ff, ln2_g[l], ln2_b[l])
    return h
```

```python
import functools
import math

import jax
import jax.numpy as jnp
from jax import lax
from jax.experimental import pallas as pl
from jax.experimental.pallas import tpu as pltpu

F32 = jnp.float32
BF16 = jnp.bfloat16

MLA_HEADS = 4
MLA_NOPE = 128
MLA_ROPE = 64
MLA_QK = MLA_NOPE + MLA_ROPE
MLA_V = 128
MLA_Q_RANK = 384
MLA_KV_RANK = 256
ROPE_THETA = 10000.0
HG_HEADS = 4
HG_DK = 64
HG_DV = 64
HG_W = HG_HEADS * HG_DK
S5_GROUP = 16
S5_GROUPS = 16
S5_STATE = 64
S5_W = S5_GROUP * S5_GROUPS
LN_EPS = 1e-5
RMS_EPS = 1e-6

HG_SUB = 16
S5_CHUNK = 16
NEG_BIG = -1e30

VMEM_LIMIT = 56 * 1024 * 1024


def _dot(a, b):
    return jnp.dot(a, b, preferred_element_type=F32)


def _dot_nt(a, b):
    return lax.dot_general(a, b, (((1,), (1,)), ((), ())), preferred_element_type=F32)


def _dot_tn(a, b):
    return lax.dot_general(a, b, (((0,), (0,)), ((), ())), preferred_element_type=F32)


def _layernorm(x, g, b):
    mu = jnp.mean(x, -1, keepdims=True)
    xc = x - mu
    var = jnp.mean(xc * xc, -1, keepdims=True)
    return xc * lax.rsqrt(var + LN_EPS) * g + b


def _rmsnorm(x, g):
    return x * lax.rsqrt(jnp.mean(x * x, -1, keepdims=True) + RMS_EPS) * g


def _const_spec(shape):
    nd = len(shape)
    return pl.BlockSpec(shape, lambda *_: (0,) * nd)


def _params(sem):
    return pltpu.CompilerParams(dimension_semantics=sem, vmem_limit_bytes=VMEM_LIMIT)


def _proj_kernel(*refs, apply_ln):
    if apply_ln:
        x_ref, g_ref, b_ref, wa_ref, wb_ref, wc_ref, h_ref, a_ref, bh_ref, c_ref = refs
        h = _layernorm(x_ref[...], g_ref[...], b_ref[...])
        h_ref[...] = h
    else:
        x_ref, wa_ref, wb_ref, wc_ref, a_ref, bh_ref, c_ref = refs
        h = x_ref[...]
    hb = h.astype(BF16)
    a_ref[...] = _dot(hb, wa_ref[...])
    bh_ref[...] = _dot(hb, wb_ref[...])
    c_ref[...] = _dot(hb, wc_ref[...])


def _in_proj(x2, ln, wa, wb, wc, tm):
    T, D = x2.shape
    row = lambda w: pl.BlockSpec((tm, w), lambda i: (i, 0))
    outs = [jax.ShapeDtypeStruct((T, wa.shape[1]), F32),
            jax.ShapeDtypeStruct((T, wb.shape[1]), F32),
            jax.ShapeDtypeStruct((T, wc.shape[1]), F32)]
    out_specs = [row(wa.shape[1]), row(wb.shape[1]), row(wc.shape[1])]
    w_specs = [_const_spec(wa.shape), _const_spec(wb.shape), _const_spec(wc.shape)]
    if ln is not None:
        g, b = ln
        args = (x2, g, b, wa, wb, wc)
        in_specs = [row(D), _const_spec(g.shape), _const_spec(b.shape)] + w_specs
        outs = [jax.ShapeDtypeStruct((T, D), F32)] + outs
        out_specs = [row(D)] + out_specs
    else:
        args = (x2, wa, wb, wc)
        in_specs = [row(D)] + w_specs
    return pl.pallas_call(
        functools.partial(_proj_kernel, apply_ln=ln is not None),
        grid=(T // tm,), in_specs=in_specs, out_specs=out_specs, out_shape=outs,
        compiler_params=_params(("parallel",)), name="in_proj")(*args)


def _mla_prep_kernel(a_ref, cos_ref, sin_ref, qg_ref, kvg_ref, wq_ref, wkv_ref,
                     q_ref, k_ref, v_ref):
    a = a_ref[...]
    cos = cos_ref[...]
    sin = sin_ref[...]
    c_q = a[:, :MLA_Q_RANK]
    c_kv = a[:, MLA_Q_RANK:MLA_Q_RANK + MLA_KV_RANK]
    o = MLA_Q_RANK + MLA_KV_RANK
    kr = a[:, o:o + MLA_ROPE]
    kr_rot = a[:, o + MLA_ROPE:o + 2 * MLA_ROPE]
    scale = MLA_QK ** -0.5
    qa = _dot(_rmsnorm(c_q, qg_ref[...]).astype(BF16), wq_ref[...])
    kva = _dot(_rmsnorm(c_kv, kvg_ref[...]).astype(BF16), wkv_ref[...])
    k_pe = (kr * cos + kr_rot * sin).astype(BF16)
    qw = MLA_NOPE + 2 * MLA_ROPE
    for h in range(MLA_HEADS):
        qh = qa[:, h * qw:(h + 1) * qw]
        q_ref[0, h, :, :MLA_NOPE] = (qh[:, :MLA_NOPE] * scale).astype(BF16)
        q_pe = qh[:, MLA_NOPE:MLA_NOPE + MLA_ROPE] * cos + qh[:, MLA_NOPE + MLA_ROPE:] * sin
        q_ref[0, h, :, MLA_NOPE:] = (q_pe * scale).astype(BF16)
        kvh = kva[:, h * (MLA_NOPE + MLA_V):(h + 1) * (MLA_NOPE + MLA_V)]
        k_ref[0, h, :, :MLA_NOPE] = kvh[:, :MLA_NOPE].astype(BF16)
        k_ref[0, h, :, MLA_NOPE:] = k_pe
        v_ref[0, h] = kvh[:, MLA_NOPE:].astype(BF16)


def _mla_prep(a2, cos, sin, qg, kvg, wq, wkv, B, S, tm):
    ns = S // tm
    H = MLA_HEADS
    return pl.pallas_call(
        _mla_prep_kernel,
        grid=(B, ns),
        in_specs=[pl.BlockSpec((tm, a2.shape[1]), lambda b, i: (b * ns + i, 0)),
                  pl.BlockSpec((tm, MLA_ROPE), lambda b, i: (i, 0)),
                  pl.BlockSpec((tm, MLA_ROPE), lambda b, i: (i, 0)),
                  _const_spec(qg.shape), _const_spec(kvg.shape),
                  _const_spec(wq.shape), _const_spec(wkv.shape)],
        out_specs=[pl.BlockSpec((1, H, tm, MLA_QK), lambda b, i: (b, 0, i, 0)),
                   pl.BlockSpec((1, H, tm, MLA_QK), lambda b, i: (b, 0, i, 0)),
                   pl.BlockSpec((1, H, tm, MLA_V), lambda b, i: (b, 0, i, 0))],
        out_shape=[jax.ShapeDtypeStruct((B, H, S, MLA_QK), BF16),
                   jax.ShapeDtypeStruct((B, H, S, MLA_QK), BF16),
                   jax.ShapeDtypeStruct((B, H, S, MLA_V), BF16)],
        compiler_params=_params(("parallel", "parallel")), name="mla_prep")(
            a2, cos, sin, qg, kvg, wq, wkv)


def _attn_kernel(q_ref, k_ref, v_ref, o_ref, *, tq):
    i = pl.program_id(2)
    q = q_ref[0, 0]

    def block(kb, carry, masked):
        m, l, acc = carry
        r0 = pl.multiple_of(kb * tq, tq)
        k = k_ref[0, 0, pl.ds(r0, tq), :]
        v = v_ref[0, 0, pl.ds(r0, tq), :]
        s = _dot_nt(q, k)
        if masked:
            row = lax.broadcasted_iota(jnp.int32, s.shape, 0)
            col = lax.broadcasted_iota(jnp.int32, s.shape, 1)
            s = jnp.where(col <= row, s, -jnp.inf)
        m_new = jnp.maximum(m, jnp.max(s, axis=-1, keepdims=True))
        p = jnp.exp(s - m_new)
        alpha = jnp.exp(m - m_new)
        l = alpha * l + jnp.sum(p, axis=-1, keepdims=True)
        acc = alpha * acc + _dot(p.astype(BF16), v)
        return m_new, l, acc

    init = (jnp.full((tq, 1), -jnp.inf, F32), jnp.zeros((tq, 1), F32),
            jnp.zeros((tq, MLA_V), F32))
    carry = lax.fori_loop(0, i, lambda kb, c: block(kb, c, False), init)
    _, l, acc = block(i, carry, True)
    o_ref[0] = (acc / l).astype(o_ref.dtype)


def _attention(q, k, v, tq):
    B, H, S, _ = q.shape
    return pl.pallas_call(
        functools.partial(_attn_kernel, tq=tq),
        grid=(B, H, S // tq),
        in_specs=[pl.BlockSpec((1, 1, tq, MLA_QK), lambda b, h, i: (b, h, i, 0)),
                  pl.BlockSpec((1, 1, S, MLA_QK), lambda b, h, i: (b, h, 0, 0)),
                  pl.BlockSpec((1, 1, S, MLA_V), lambda b, h, i: (b, h, 0, 0))],
        out_specs=pl.BlockSpec((1, tq, MLA_V), lambda b, h, i: (b, i, h)),
        out_shape=jax.ShapeDtypeStruct((B, S, H * MLA_V), BF16),
        compiler_params=_params(("parallel", "parallel", "arbitrary")), name="mla_attn")(q, k, v)


def _head_block_mask(n, dtype):
    r = lax.broadcasted_iota(jnp.int32, (n, n), 0) // HG_DK
    c = lax.broadcasted_iota(jnp.int32, (n, n), 1) // HG_DK
    return (r == c).astype(dtype)


def _hgrn2_kernel(x_ref, lb_ref, ng_ref, o_ref, st_ref, q_s, k_s, b_s, v_s, o_s, *, blk):
    W = HG_W

    @pl.when(pl.program_id(1) == 0)
    def _():
        st_ref[...] = jnp.zeros_like(st_ref)

    x = x_ref[...]
    q = x[:, :W]
    z = x[:, W:2 * W]
    v = x[:, 2 * W:3 * W]
    g = x[:, 3 * W:]
    lb = lb_ref[...]
    log_sig = jnp.minimum(z, 0.0) - jnp.log1p(jnp.exp(-jnp.abs(z)))
    a0 = jnp.log(lb)
    a1 = jnp.log1p(-lb) + log_sig
    amax = jnp.maximum(a0, a1)
    lf = amax + jnp.log1p(jnp.exp(-jnp.abs(a0 - a1)))
    key = (1.0 - lb) * (1.0 / (1.0 + jnp.exp(z)))
    r = lax.broadcasted_iota(jnp.int32, (blk, blk), 0)
    c = lax.broadcasted_iota(jnp.int32, (blk, blk), 1)
    tril = ((r // HG_SUB == c // HG_SUB) & (c <= r)).astype(F32)
    q_s[...] = q
    k_s[...] = key
    v_s[...] = v
    b_s[...] = jnp.dot(tril, lf, preferred_element_type=F32, precision=lax.Precision.HIGHEST)

    ones_bd = _head_block_mask(W, BF16)
    mask_bd = _head_block_mask(W, F32)
    rows = lax.broadcasted_iota(jnp.int32, (HG_SUB, 1), 0)

    def sub(i, carry):
        r0 = pl.multiple_of(i * HG_SUB, HG_SUB)
        qi = q_s[pl.ds(r0, HG_SUB), :]
        ki = k_s[pl.ds(r0, HG_SUB), :]
        bi = b_s[pl.ds(r0, HG_SUB), :]
        vi = v_s[pl.ds(r0, HG_SUB), :]
        b_end = bi[HG_SUB - 1:HG_SUB, :]
        st = st_ref[...]
        o_i = _dot_nt((qi * jnp.exp(bi)).astype(BF16), st.astype(BF16))
        kt = (ki * jnp.exp(b_end - bi)).astype(BF16)
        upd = _dot_tn(vi.astype(BF16), kt) * mask_bd
        st_ref[...] = st * jnp.exp(b_end) + upd
        ps = []
        for s in range(HG_SUB):
            d = jnp.where(rows >= s, bi - bi[s:s + 1, :], NEG_BIG)
            ps.append((qi * ki[s:s + 1, :] * jnp.exp(d)).astype(BF16))
        red = _dot(jnp.concatenate(ps, axis=0), ones_bd)
        for s in range(HG_SUB):
            o_i = o_i + red[s * HG_SUB:(s + 1) * HG_SUB, :] * vi[s:s + 1, :]
        o_s[pl.ds(r0, HG_SUB), :] = o_i
        return carry

    lax.fori_loop(0, blk // HG_SUB, sub, 0)

    o = o_s[...]
    ms = _dot((o * o).astype(BF16), ones_bd) * (1.0 / HG_DV)
    o = o * lax.rsqrt(ms + RMS_EPS) * ng_ref[...]
    o_ref[...] = (o * (g / (1.0 + jnp.exp(-g)))).astype(o_ref.dtype)


def _hgrn2(bh2, lb, ng, B, S, blk):
    ns = S // blk
    W = HG_W
    return pl.pallas_call(
        functools.partial(_hgrn2_kernel, blk=blk),
        grid=(B, ns),
        in_specs=[pl.BlockSpec((blk, 4 * W), lambda b, i: (b * ns + i, 0)),
                  _const_spec(lb.shape), _const_spec(ng.shape)],
        out_specs=pl.BlockSpec((blk, W), lambda b, i: (b * ns + i, 0)),
        out_shape=jax.ShapeDtypeStruct((B * S, W), BF16),
        scratch_shapes=[pltpu.VMEM((W, W), F32)] + [pltpu.VMEM((blk, W), F32)] * 5,
        compiler_params=_params(("parallel", "arbitrary")), name="hgrn2")(bh2, lb, ng)


def _s5_kernel(u_ref, t_ref, w_ref, v_ref, ap_ref, y_ref, *, nc, nsteps):
    u = u_ref[0].astype(BF16)
    h = _dot(u, w_ref[0])
    half = S5_STATE
    rows = lax.broadcasted_iota(jnp.int32, (nc, 1), 0)
    for j in range(nsteps):
        d = 1 << j
        sh = jnp.where(rows >= d, pltpu.roll(h, d, 0), 0.0)
        sw = pltpu.roll(sh, half, 1)
        h = h + ap_ref[0, 2 * j:2 * j + 1, :] * sh + ap_ref[0, 2 * j + 1:2 * j + 2, :] * sw
    hs = jnp.where(rows >= 1, pltpu.roll(h, 1, 0), 0.0)
    y_ref[0] = _dot(u, t_ref[0]) + _dot(hs.astype(BF16), v_ref[0])


def _s5_scan(ug, tg, wg, vg, ap, B, nc):
    G = ug.shape[0]
    nsteps = ap.shape[1] // 2
    return pl.pallas_call(
        functools.partial(_s5_kernel, nc=nc, nsteps=nsteps),
        grid=(G, B),
        in_specs=[pl.BlockSpec((1, nc, S5_W), lambda g, b: (g, b, 0)),
                  pl.BlockSpec((1,) + tg.shape[1:], lambda g, b: (g, 0, 0)),
                  pl.BlockSpec((1,) + wg.shape[1:], lambda g, b: (g, 0, 0)),
                  pl.BlockSpec((1,) + vg.shape[1:], lambda g, b: (g, 0, 0)),
                  pl.BlockSpec((1,) + ap.shape[1:], lambda g, b: (g, 0, 0))],
        out_specs=pl.BlockSpec((1, nc, S5_W), lambda g, b: (g, b, 0)),
        out_shape=jax.ShapeDtypeStruct((G, B * nc, S5_W), F32),
        compiler_params=_params(("parallel", "parallel")), name="s5_scan")(ug, tg, wg, vg, ap)


def _s5_constants(a_re, a_im, b_re, b_im, c_re, c_im, log_dt, nsteps):
    hp = lax.Precision.HIGHEST
    C = S5_CHUNK
    dt = jnp.exp(log_dt)[:, None]
    mag = jnp.exp(dt * a_re)
    ab_re, ab_im = mag * jnp.cos(dt * a_im), mag * jnp.sin(dt * a_im)
    den = a_re * a_re + a_im * a_im
    num_re, num_im = ab_re - 1.0, ab_im
    coef_re = ((num_re * a_re + num_im * a_im) / den)[..., None]
    coef_im = ((num_im * a_re - num_re * a_im) / den)[..., None]
    bb_re = coef_re * b_re - coef_im * b_im
    bb_im = coef_re * b_im + coef_im * b_re
    pw_re, pw_im = [jnp.ones_like(ab_re)], [jnp.zeros_like(ab_re)]
    for _ in range(C):
        pr, pi = pw_re[-1], pw_im[-1]
        pw_re.append(pr * ab_re - pi * ab_im)
        pw_im.append(pr * ab_im + pi * ab_re)
    pw_re, pw_im = jnp.stack(pw_re), jnp.stack(pw_im)
    ca_re = c_re[None] * pw_re[:, :, None, :] - c_im[None] * pw_im[:, :, None, :]
    ca_im = c_re[None] * pw_im[:, :, None, :] + c_im[None] * pw_re[:, :, None, :]
    kern = (jnp.einsum('jgqn,gnp->jgqp', ca_re[:C], bb_re, precision=hp)
            - jnp.einsum('jgqn,gnp->jgqp', ca_im[:C], bb_im, precision=hp))
    s_idx = jnp.arange(C)[:, None]
    t_idx = jnp.arange(C)[None, :]
    lag = jnp.clip(t_idx - s_idx, 0, C - 1)
    tg = jnp.where((t_idx >= s_idx)[:, :, None, None, None], kern[lag], 0.0)
    tg = tg.transpose(2, 0, 4, 1, 3).reshape(-1, C * S5_GROUP, C * S5_GROUP)
    rev = pw_re[C - 1::-1][:C], pw_im[C - 1::-1][:C]
    wre = rev[0][:, :, :, None] * bb_re[None] - rev[1][:, :, :, None] * bb_im[None]
    wim = rev[0][:, :, :, None] * bb_im[None] + rev[1][:, :, :, None] * bb_re[None]
    wg = jnp.concatenate([wre, wim], axis=2).transpose(1, 0, 3, 2)
    wg = wg.reshape(-1, C * S5_GROUP, 2 * S5_STATE)
    vre = ca_re[1:].transpose(1, 3, 0, 2)
    vim = -ca_im[1:].transpose(1, 3, 0, 2)
    vg = jnp.concatenate([vre, vim], axis=1).reshape(-1, 2 * S5_STATE, C * S5_GROUP)
    pr, pi = pw_re[C], pw_im[C]
    rows = []
    for _ in range(nsteps):
        rows.append(jnp.concatenate([pr, pr], axis=-1))
        rows.append(jnp.concatenate([-pi, pi], axis=-1))
        pr, pi = pr * pr - pi * pi, 2.0 * pr * pi
    ap = jnp.stack(rows, axis=1)
    return tg.astype(BF16), wg.astype(BF16), vg.astype(BF16), ap.astype(F32)


def _mix_ffn_kernel(h_ref, om_ref, oh_ref, ys_ref, su_ref, dsk_ref, wglu_ref, bglu_ref,
                    wo_ref, g1_ref, b1_ref, w1_ref, w2_ref, g2_ref, b2_ref, o_ref,
                    *, alpha, ff_chunk):
    y = jax.nn.gelu(ys_ref[...] + dsk_ref[...] * su_ref[...])
    gate = _dot(y.astype(BF16), wglu_ref[...]) + bglu_ref[...]
    o_s5 = (y * (1.0 / (1.0 + jnp.exp(-gate)))).astype(BF16)
    n_mla = om_ref.shape[1]
    n_hg = oh_ref.shape[1]
    mix = (_dot(om_ref[...], wo_ref[:n_mla, :])
           + _dot(oh_ref[...], wo_ref[n_mla:n_mla + n_hg, :])
           + _dot(o_s5, wo_ref[n_mla + n_hg:, :]))
    h1 = _layernorm(alpha * h_ref[...] + mix, g1_ref[...], b1_ref[...])
    hb = h1.astype(BF16)
    d_ff = w1_ref.shape[1]
    ff = jnp.zeros_like(h1)
    for c in range(d_ff // ff_chunk):
        a = jnp.maximum(_dot(hb, w1_ref[:, c * ff_chunk:(c + 1) * ff_chunk]), 0.0)
        ff = ff + _dot((a * a).astype(BF16), w2_ref[c * ff_chunk:(c + 1) * ff_chunk, :])
    o_ref[...] = _layernorm(alpha * h1 + ff, g2_ref[...], b2_ref[...])


def _mix_ffn(h2, o_mla, o_hg, y_s5, su, dsk, wglu, bglu, wo, g1, b1, w1, w2, g2, b2,
             alpha, tm, ff_chunk):
    T, D = h2.shape
    row = lambda w: pl.BlockSpec((tm, w), lambda i: (i, 0))
    consts = (dsk, wglu, bglu, wo, g1, b1, w1, w2, g2, b2)
    return pl.pallas_call(
        functools.partial(_mix_ffn_kernel, alpha=alpha, ff_chunk=ff_chunk),
        grid=(T // tm,),
        in_specs=[row(D), row(o_mla.shape[1]), row(o_hg.shape[1]), row(y_s5.shape[1]),
                  row(su.shape[1])] + [_const_spec(c.shape) for c in consts],
        out_specs=row(D),
        out_shape=jax.ShapeDtypeStruct((T, D), F32),
        compiler_params=_params(("parallel",)), name="mix_ffn")(
            h2, o_mla, o_hg, y_s5, su, *consts)


def _rot_half_cols(w):
    r = w.shape[-1] // 2
    return jnp.concatenate([-w[..., r:], w[..., :r]], axis=-1)


def kernel(x, ln_in_g, ln_in_b, w_in, mla_q_norm_g, mla_w_uq, mla_kv_norm_g, mla_w_ukv, hg_lower_bound, hg_norm_g, s5_a_re, s5_a_im, s5_b_re, s5_b_im, s5_c_re, s5_c_im, s5_d, s5_log_dt, s5_w_glu, s5_b_glu, w_out, ln1_g, ln1_b, w_ff1, w_ff2, ln2_g, ln2_b):
    B, S, D = x.shape
    depth = w_in.shape[0]
    T = B * S
    alpha = (2 * depth) ** 0.25
    tm = min(512, S)
    tq = min(512, S)
    hg_blk = min(256, S)
    nc = S // S5_CHUNK
    nsteps = max(1, (nc - 1).bit_length())
    row = lambda a: a.reshape(1, -1)

    pos = jnp.arange(S, dtype=F32)
    freqs = ROPE_THETA ** (-jnp.arange(0, MLA_ROPE, 2, dtype=F32) / MLA_ROPE)
    ang = pos[:, None] * freqs[None, :]
    cos = jnp.concatenate([jnp.cos(ang), jnp.cos(ang)], axis=-1)
    sin = jnp.concatenate([jnp.sin(ang), jnp.sin(ang)], axis=-1)

    lb_all = jnp.cumsum(jax.nn.softmax(hg_lower_bound.astype(F32), axis=0), axis=0)
    lb_all = lb_all - lb_all[0]

    o_kr = MLA_Q_RANK + MLA_KV_RANK
    o_hg = o_kr + MLA_ROPE
    o_s5 = o_hg + 4 * HG_W

    h2 = x.reshape(T, D)
    for l in range(depth):
        w = w_in[l]
        wa = jnp.concatenate([w[:, :o_hg], _rot_half_cols(w[:, o_kr:o_hg])], axis=1).astype(BF16)
        wb = w[:, o_hg:o_s5].astype(BF16)
        wc = w[:, o_s5:].astype(BF16)
        if l == 0:
            h2, a2, bh2, su = _in_proj(h2, (row(ln_in_g), row(ln_in_b)), wa, wb, wc, tm)
        else:
            a2, bh2, su = _in_proj(h2, None, wa, wb, wc, tm)

        wq = mla_w_uq[l].reshape(MLA_Q_RANK, MLA_HEADS, MLA_QK)
        wq = jnp.concatenate([wq, _rot_half_cols(wq[..., MLA_NOPE:])], axis=-1)
        wq = wq.reshape(MLA_Q_RANK, -1).astype(BF16)
        q, k, v = _mla_prep(a2, cos, sin, row(mla_q_norm_g[l]), row(mla_kv_norm_g[l]),
                            wq, mla_w_ukv[l].astype(BF16), B, S, tm)
        o_mla = _attention(q, k, v, tq).reshape(T, MLA_HEADS * MLA_V)

        o_hgrn = _hgrn2(bh2, row(lb_all[l]), row(jnp.tile(hg_norm_g[l], HG_HEADS)), B, S, hg_blk)

        tg, wg, vg, ap = _s5_constants(s5_a_re[l], s5_a_im[l], s5_b_re[l], s5_b_im[l],
                                       s5_c_re[l], s5_c_im[l], s5_log_dt[l], nsteps)
        ug = su.reshape(B * nc, S5_CHUNK, S5_GROUPS, S5_GROUP).transpose(2, 0, 1, 3)
        ug = ug.reshape(S5_GROUPS, B * nc, S5_W)
        yg = _s5_scan(ug, tg, wg, vg, ap, B, nc)
        y_s5 = yg.reshape(S5_GROUPS, B * nc, S5_CHUNK, S5_GROUP).transpose(1, 2, 0, 3)
        y_s5 = y_s5.reshape(T, S5_W)

        h2 = _mix_ffn(h2, o_mla, o_hgrn, y_s5, su, row(s5_d[l]), s5_w_glu[l].astype(BF16),
                      row(s5_b_glu[l]), w_out[l].astype(BF16), row(ln1_g[l]), row(ln1_b[l]),
                      w_ff1[l].astype(BF16), w_ff2[l].astype(BF16), row(ln2_g[l]), row(ln2_b[l]),
                      alpha, tm, 1024)
    return h2.reshape(B, S, D)
```
